```python
import jax, jax.numpy as jnp
from jax import lax
import numpy as np

D_MODEL = 1024
BATCH = 4
SEQ = 8192
DEPTH = 1

HEAD_DIM = 64
N_Q_HEADS = 8
N_KV_HEADS = 2
Q_PER_KV = N_Q_HEADS // N_KV_HEADS
ATTN_WIDTH = N_Q_HEADS * HEAD_DIM
KV_WIDTH = N_KV_HEADS * HEAD_DIM
WINDOW = 128
BLOCK = 128
ROPE_THETA = 10000.0
CONV_WIDTH = D_MODEL - ATTN_WIDTH
CONV_KSIZE = 3
MIX_WIDTH = ATTN_WIDTH + CONV_WIDTH
IN_COLS = ATTN_WIDTH + 2 * KV_WIDTH + 3 * CONV_WIDTH
D_FF = 2816
FFN_RESIDUAL_WEIGHT = 0.5
MEM_LEN = 256
X_HEADS = 4
X_HEAD_DIM = D_MODEL // X_HEADS

RMS_EPS = 1e-5
NEG_INF = -1e30
MAX_POS_OFFSET = 4096

kernel_name = "hymba_swa_sink_shortconv_macaron_memxattn"


def rms_norm(x, gain):
    xf = x.astype(jnp.float32)
    y = xf * lax.rsqrt(jnp.mean(xf * xf, axis=-1, keepdims=True) + RMS_EPS)
    return (y * gain.astype(jnp.float32)).astype(x.dtype)


def swiglu(u, w_in, w_out):
    gate, up = jnp.split(u @ w_in, 2, axis=-1)
    return (jax.nn.silu(gate) * up) @ w_out


def rope(t, positions):
    half = HEAD_DIM // 2
    inv_freq = ROPE_THETA ** (-jnp.arange(half, dtype=jnp.float32) / half)
    ang = positions.astype(jnp.float32)[..., None] * inv_freq
    cos = jnp.cos(ang)[:, :, None, :]
    sin = jnp.sin(ang)[:, :, None, :]
    tf = t.astype(jnp.float32)
    t1, t2 = tf[..., :half], tf[..., half:]
    out = jnp.concatenate([t1 * cos - t2 * sin, t2 * cos + t1 * sin], axis=-1)
    return out.astype(t.dtype)


def sliding_window_attention(q, k, v, sinks):
    b, s = q.shape[0], q.shape[1]
    nb = s // BLOCK
    qb = q.reshape(b, nb, BLOCK, N_KV_HEADS, Q_PER_KV, HEAD_DIM)
    kb = k.reshape(b, nb, BLOCK, N_KV_HEADS, HEAD_DIM)
    vb = v.reshape(b, nb, BLOCK, N_KV_HEADS, HEAD_DIM)

    def band(t):
        prev = jnp.pad(t, ((0, 0), (1, 0), (0, 0), (0, 0), (0, 0)))[:, :-1]
        return jnp.concatenate([prev, t], axis=2)

    k_band, v_band = band(kb), band(vb)
    scores = jnp.einsum('bnqhgd,bnkhd->bnhgqk', qb, k_band).astype(jnp.float32) * (HEAD_DIM ** -0.5)
    qi = jnp.arange(BLOCK)[:, None]
    ki = jnp.arange(2 * BLOCK)[None, :]
    rel = BLOCK + qi - ki
    in_window = (rel >= 0) & (rel < WINDOW)
    blk = jnp.arange(nb)[:, None, None]
    valid = in_window[None] & ((blk > 0) | (ki[None] >= BLOCK))
    scores = jnp.where(valid[None, :, None, None], scores, NEG_INF)
    sink = sinks.astype(jnp.float32).reshape(N_KV_HEADS, Q_PER_KV)[None, None, :, :, None, None]
    sink = jnp.broadcast_to(sink, scores.shape[:-1] + (1,))
    probs = jax.nn.softmax(jnp.concatenate([scores, sink], axis=-1), axis=-1)[..., :-1]
    out = jnp.einsum('bnhgqk,bnkhd->bnqhgd', probs.astype(v.dtype), v_band)
    return out.reshape(b, s, ATTN_WIDTH)


def short_gated_conv(xc, gate_b, gate_c, conv_w):
    z = gate_c * xc
    rhs = conv_w[:, None, :].astype(z.dtype)
    conv = lax.conv_general_dilated(z, rhs, window_strides=(1,), padding=[(CONV_KSIZE - 1, 0)],
                                    dimension_numbers=('NWC', 'WIO', 'NWC'),
                                    feature_group_count=CONV_WIDTH)
    return gate_b * conv


def memory_cross_attention(u, mem_n, w_xq, w_xkv, w_xo):
    b, s, _ = u.shape
    m = mem_n.shape[1]
    q = (u @ w_xq).reshape(b, s, X_HEADS, X_HEAD_DIM)
    k, v = jnp.split(mem_n @ w_xkv, 2, axis=-1)
    k = k.reshape(b, m, X_HEADS, X_HEAD_DIM)
    v = v.reshape(b, m, X_HEADS, X_HEAD_DIM)
    scores = jnp.einsum('bshd,bmhd->bhsm', q, k).astype(jnp.float32) * (X_HEAD_DIM ** -0.5)
    probs = jax.nn.softmax(scores, axis=-1)
    o = jnp.einsum('bhsm,bmhd->bshd', probs.astype(v.dtype), v).reshape(b, s, D_MODEL)
    return o @ w_xo


def setup_inputs(seed: int = 0) -> dict:
    key = jax.random.key(seed)
    ks = jax.random.split(key, 24)
    f32 = jnp.float32

    def w(k, shape, fan_in):
        return jax.random.normal(k, shape, f32) * (fan_in ** -0.5)

    def gain(k, shape):
        return 1.0 + 0.02 * jax.random.normal(k, shape, f32)

    x = jax.random.normal(ks[0], (BATCH, SEQ, D_MODEL), f32)
    mem = jax.random.normal(ks[1], (BATCH, MEM_LEN, D_MODEL), f32)
    offsets = jax.random.randint(ks[2], (BATCH, 1), 0, MAX_POS_OFFSET, dtype=jnp.int32)
    positions = (jnp.arange(SEQ, dtype=jnp.int32)[None, :] + offsets).astype(jnp.int32)
    return {
        "x": x,
        "mem": mem,
        "positions": positions,
        "g_ffn1": gain(ks[3], (DEPTH, D_MODEL)),
        "w_ffn1_in": w(ks[4], (DEPTH, D_MODEL, 2 * D_FF), D_MODEL),
        "w_ffn1_out": w(ks[5], (DEPTH, D_FF, D_MODEL), D_FF),
        "g_mix": gain(ks[6], (DEPTH, D_MODEL)),
        "w_mix_in": w(ks[7], (DEPTH, D_MODEL, IN_COLS), D_MODEL),
        "sinks": 0.5 * jax.random.normal(ks[8], (DEPTH, N_Q_HEADS), f32),
        "conv_w": w(ks[9], (DEPTH, CONV_KSIZE, CONV_WIDTH), CONV_KSIZE),
        "g_attn_out": gain(ks[10], (DEPTH, ATTN_WIDTH)),
        "g_conv_out": gain(ks[11], (DEPTH, CONV_WIDTH)),
        "w_mix_out": w(ks[12], (DEPTH, MIX_WIDTH, D_MODEL), MIX_WIDTH),
        "g_mem": gain(ks[13], (DEPTH, D_MODEL)),
        "g_xattn": gain(ks[14], (DEPTH, D_MODEL)),
        "w_xq": w(ks[15], (DEPTH, D_MODEL, D_MODEL), D_MODEL),
        "w_xkv": w(ks[16], (DEPTH, D_MODEL, 2 * D_MODEL), D_MODEL),
        "w_xo": w(ks[17], (DEPTH, D_MODEL, D_MODEL), D_MODEL),
        "g_ffn2": gain(ks[18], (DEPTH, D_MODEL)),
        "w_ffn2_in": w(ks[19], (DEPTH, D_MODEL, 2 * D_FF), D_MODEL),
        "w_ffn2_out": w(ks[20], (DEPTH, D_FF, D_MODEL), D_FF),
        "g_final": gain(ks[21], (D_MODEL,)),
    }


def reference(x, mem, positions, g_ffn1, w_ffn1_in, w_ffn1_out, g_mix, w_mix_in, sinks, conv_w,
              g_attn_out, g_conv_out, w_mix_out, g_mem, g_xattn, w_xq, w_xkv, w_xo,
              g_ffn2, w_ffn2_in, w_ffn2_out, g_final):
    b, s, _ = x.shape
    h = x
    for l in range(DEPTH):
        h = h + FFN_RESIDUAL_WEIGHT * swiglu(rms_norm(h, g_ffn1[l]), w_ffn1_in[l], w_ffn1_out[l])

        u = rms_norm(h, g_mix[l])
        proj = u @ w_mix_in[l]
        o0 = ATTN_WIDTH
        o1 = o0 + KV_WIDTH
        o2 = o1 + KV_WIDTH
        o3 = o2 + CONV_WIDTH
        o4 = o3 + CONV_WIDTH
        q = rope(proj[..., :o0].reshape(b, s, N_Q_HEADS, HEAD_DIM), positions)
        k = rope(proj[..., o0:o1].reshape(b, s, N_KV_HEADS, HEAD_DIM), positions)
        v = proj[..., o1:o2].reshape(b, s, N_KV_HEADS, HEAD_DIM)
        gate_b = proj[..., o2:o3]
        gate_c = proj[..., o3:o4]
        xc = proj[..., o4:]

        attn = sliding_window_attention(q, k, v, sinks[l])
        conv = short_gated_conv(xc, gate_b, gate_c, conv_w[l])
        mixed = jnp.concatenate([rms_norm(attn, g_attn_out[l]), rms_norm(conv, g_conv_out[l])], axis=-1)
        h = h + mixed @ w_mix_out[l]

        h = h + memory_cross_attention(rms_norm(h, g_xattn[l]), rms_norm(mem, g_mem[l]),
                                       w_xq[l], w_xkv[l], w_xo[l])

        h = h + FFN_RESIDUAL_WEIGHT * swiglu(rms_norm(h, g_ffn2[l]), w_ffn2_in[l], w_ffn2_out[l])
    return rms_norm(h, g_final)
```

```python
import functools

import jax
import jax.numpy as jnp
from jax.experimental import pallas as pl
from jax.experimental.pallas import tpu as pltpu

HEAD_DIM = 64
N_Q_HEADS = 8
N_KV_HEADS = 2
Q_PER_KV = N_Q_HEADS // N_KV_HEADS
WINDOW = 128
BLOCK = 128
ROPE_THETA = 10000.0
CONV_KSIZE = 3
X_HEADS = 4
RMS_EPS = 1e-5
NEG_INF = -1e30
FFN_RESIDUAL_WEIGHT = 0.5

LANES = 128
SUBLANES = 8
MXU_DIM = 256
VMEM_CAPACITY = 64 * 1024 * 1024

TOKEN_TILE = 512
FF_CHUNK = 4 * MXU_DIM


def _vmem_limit(resident_bytes, streamed_bytes, temp_bytes):
    need = resident_bytes + 2 * streamed_bytes + temp_bytes
    return int(min(VMEM_CAPACITY - 4 * 1024 * 1024, need + need // 4))


def _resident(shape):
    zeros = (0,) * len(shape)
    return pl.BlockSpec(shape, lambda *_: zeros, pipeline_mode=pl.Buffered(1))


def _rows(tile, width):
    return pl.BlockSpec((tile, width), lambda i: (i, 0))


def _rms_norm(x, gain):
    return x * jax.lax.rsqrt(jnp.mean(x * x, axis=-1, keepdims=True) + RMS_EPS) * gain


def _dot(a, b):
    return jnp.dot(a, b, preferred_element_type=jnp.float32)


def _ffn_kernel(h_ref, g_ref, wg_ref, wu_ref, wo_ref, gf_ref, o_ref, *, d_ff, final_norm):
    x = h_ref[...]
    u = _rms_norm(x, g_ref[...]).astype(jnp.bfloat16)
    acc = jnp.zeros(x.shape, jnp.float32)
    for c0 in range(0, d_ff, FF_CHUNK):
        c1 = min(c0 + FF_CHUNK, d_ff)
        gate = _dot(u, wg_ref[:, c0:c1])
        up = _dot(u, wu_ref[:, c0:c1])
        act = (gate * (1.0 / (1.0 + jnp.exp(-gate))) * up).astype(jnp.bfloat16)
        acc = acc + _dot(act, wo_ref[c0:c1, :])
    y = x + FFN_RESIDUAL_WEIGHT * acc
    if final_norm:
        y = _rms_norm(y, gf_ref[...])
    o_ref[...] = y


def _ffn(h, gain, w_gate, w_up, w_out, final_gain, *, final_norm):
    n, d = h.shape
    d_ff = w_gate.shape[1]
    tile = TOKEN_TILE
    weights = 2 * (w_gate.size + w_up.size + w_out.size)
    streamed = 2 * tile * d * 4
    temps = tile * (3 * FF_CHUNK * 4 + 3 * d * 4)
    return pl.pallas_call(
        functools.partial(_ffn_kernel, d_ff=d_ff, final_norm=final_norm),
        grid=(n // tile,),
        in_specs=[
            _rows(tile, d),
            _resident((1, d)),
            _resident(w_gate.shape),
            _resident(w_up.shape),
            _resident(w_out.shape),
            _resident((1, d)),
        ],
        out_specs=_rows(tile, d),
        out_shape=jax.ShapeDtypeStruct((n, d), jnp.float32),
        compiler_params=pltpu.CompilerParams(
            dimension_semantics=("arbitrary",),
            vmem_limit_bytes=_vmem_limit(weights, streamed, temps)),
        name="ffn_final" if final_norm else "ffn",
    )(h, gain, w_gate, w_up, w_out, final_gain)


def _swap_halves(t, first_half):
    return jnp.where(first_half, pltpu.roll(t, LANES - HEAD_DIM // 2, 1), pltpu.roll(t, HEAD_DIM // 2, 1))


def _mix_in_kernel(h_ref, g_ref, w_ref, pos_ref, freq_ref,
                   q_ref, k_ref, v_ref, gb_ref, z_ref, *, attn_w, kv_w, conv_w):
    u = _rms_norm(h_ref[...], g_ref[...]).astype(jnp.bfloat16)
    proj = _dot(u, w_ref[...])

    ang = pos_ref[...].astype(jnp.float32) * freq_ref[...]
    lane = jax.lax.broadcasted_iota(jnp.int32, ang.shape, 1)
    first_half = (lane % HEAD_DIM) < (HEAD_DIM // 2)
    cos = jnp.cos(ang)
    sin = jnp.sin(ang)
    sin = jnp.where(first_half, -sin, sin)

    def rope(t):
        return t * cos + _swap_halves(t, first_half) * sin

    q_scale = HEAD_DIM ** -0.5
    for s in range(attn_w // LANES):
        t = proj[:, s * LANES:(s + 1) * LANES]
        q_ref[:, s * LANES:(s + 1) * LANES] = (rope(t) * q_scale).astype(jnp.bfloat16)
    o0 = attn_w
    k_ref[...] = rope(proj[:, o0:o0 + kv_w]).astype(jnp.bfloat16)
    o1 = o0 + kv_w
    v_ref[...] = proj[:, o1:o1 + kv_w].astype(jnp.bfloat16)
    o2 = o1 + kv_w
    gb_ref[...] = proj[:, o2:o2 + conv_w]
    o3 = o2 + conv_w
    o4 = o3 + conv_w
    z_ref[...] = proj[:, o3:o4] * proj[:, o4:o4 + conv_w]


def _mix_in(h, gain, w, pos, freq, *, attn_w, kv_w, conv_w):
    n, d = h.shape
    tile = TOKEN_TILE
    cols = w.shape[1]
    out_row_bytes = 2 * (attn_w + 2 * kv_w) + 4 * 2 * conv_w
    streamed = tile * (d * 4 + LANES * 4 + out_row_bytes)
    temps = tile * (cols * 4 + d * 6 + 8 * LANES * 4)
    return pl.pallas_call(
        functools.partial(_mix_in_kernel, attn_w=attn_w, kv_w=kv_w, conv_w=conv_w),
        grid=(n // tile,),
        in_specs=[
            _rows(tile, d),
            _resident((1, d)),
            _resident(w.shape),
            _rows(tile, 1),
            _resident((1, LANES)),
        ],
        out_specs=[
            _rows(tile, attn_w), _rows(tile, kv_w), _rows(tile, kv_w),
            _rows(tile, conv_w), _rows(tile, conv_w),
        ],
        out_shape=[
            jax.ShapeDtypeStruct((n, attn_w), jnp.bfloat16),
            jax.ShapeDtypeStruct((n, kv_w), jnp.bfloat16),
            jax.ShapeDtypeStruct((n, kv_w), jnp.bfloat16),
            jax.ShapeDtypeStruct((n, conv_w), jnp.float32),
            jax.ShapeDtypeStruct((n, conv_w), jnp.float32),
        ],
        compiler_params=pltpu.CompilerParams(
            dimension_semantics=("arbitrary",),
            vmem_limit_bytes=_vmem_limit(2 * w.size, streamed, temps)),
        name="mix_in",
    )(h, gain, w, pos, freq)


def _memkv_kernel(mem_ref, g_ref, w_ref, kt_ref, v_ref, *, d):
    mem_n = _rms_norm(mem_ref[0], g_ref[...]).astype(jnp.bfloat16)
    kv = _dot(mem_n, w_ref[...])
    kt_ref[0] = kv[:, :d].T.astype(jnp.bfloat16)
    v_ref[0] = kv[:, d:].astype(jnp.bfloat16)


def _memkv(mem, gain, w):
    b, m, d = mem.shape
    return pl.pallas_call(
        functools.partial(_memkv_kernel, d=d),
        grid=(b,),
        in_specs=[
            pl.BlockSpec((1, m, d), lambda i: (i, 0, 0)),
            _resident((1, d)),
            _resident(w.shape),
        ],
        out_specs=[
            pl.BlockSpec((1, d, m), lambda i: (i, 0, 0)),
            pl.BlockSpec((1, m, d), lambda i: (i, 0, 0)),
        ],
        out_shape=[
            jax.ShapeDtypeStruct((b, d, m), jnp.bfloat16),
            jax.ShapeDtypeStruct((b, m, d), jnp.bfloat16),
        ],
        compiler_params=pltpu.CompilerParams(
            dimension_semantics=("arbitrary",),
            vmem_limit_bytes=_vmem_limit(2 * w.size, m * d * 8, m * d * 24)),
        name="memkv",
    )(mem, gain, w)


def _mix_kernel(sink_ref, h_ref, q_ref, k_ref, kp_ref, v_ref, vp_ref, gb_ref, z_ref, zp_ref,
                cw_ref, ga_ref, gc_ref, wo_ref, o_ref, attn_ref, *, tiles_per_seq, attn_w):
    tile = h_ref.shape[0]
    first_tile = (pl.program_id(0) % tiles_per_seq) == 0

    lane = jax.lax.broadcasted_iota(jnp.int32, (BLOCK, LANES), 1)
    low = lane < HEAD_DIM
    qi = jax.lax.broadcasted_iota(jnp.int32, (BLOCK, 2 * BLOCK), 0)
    ki = jax.lax.broadcasted_iota(jnp.int32, (BLOCK, 2 * BLOCK), 1)
    rel = BLOCK + qi - ki
    in_window = (rel >= 0) & (rel < WINDOW)
    in_window_first = in_window & ((ki >= BLOCK) | jnp.logical_not(first_tile))
    zero_bf = jnp.zeros((), jnp.bfloat16)

    for j in range(tile // BLOCK):
        r0 = j * BLOCK
        if j == 0:
            k_prev, v_prev, valid = kp_ref[...], vp_ref[...], in_window_first
        else:
            k_prev, v_prev, valid = k_ref[r0 - BLOCK:r0, :], v_ref[r0 - BLOCK:r0, :], in_window
        k_band = jnp.concatenate([k_prev, k_ref[r0:r0 + BLOCK, :]], axis=0)
        v_band = jnp.concatenate([v_prev, v_ref[r0:r0 + BLOCK, :]], axis=0)
        lane_kv = jax.lax.broadcasted_iota(jnp.int32, v_band.shape, 1)
        v_diag = jnp.concatenate([jnp.where(lane_kv < HEAD_DIM, v_band, zero_bf),
                                  jnp.where(lane_kv < HEAD_DIM, zero_bf, v_band)], axis=0)
        for s in range(attn_w // LANES):
            q = q_ref[r0:r0 + BLOCK, s * LANES:(s + 1) * LANES]
            q2 = jnp.concatenate([jnp.where(low, q, zero_bf), jnp.where(low, zero_bf, q)], axis=0)
            sc = jax.lax.dot_general(q2, k_band, (((1,), (1,)), ((), ())),
                                     preferred_element_type=jnp.float32)
            probs = []
            for half in range(2):
                head = half * Q_PER_KV + s
                sink = sink_ref[head]
                sh = jnp.where(valid, sc[half * BLOCK:(half + 1) * BLOCK, :], NEG_INF)
                m = jnp.maximum(jnp.max(sh, axis=-1, keepdims=True), sink)
                p = jnp.exp(sh - m)
                denom = jnp.sum(p, axis=-1, keepdims=True) + jnp.exp(sink - m)
                probs.append((p, 1.0 / denom))
            p_cat = jnp.concatenate([probs[0][0], probs[1][0]], axis=1).astype(jnp.bfloat16)
            pv = _dot(p_cat, v_diag)
            attn_ref[r0:r0 + BLOCK, s * LANES:(s + 1) * LANES] = pv * jnp.where(low, probs[0][1], probs[1][1])

    z = z_ref[...]
    row = jax.lax.broadcasted_iota(jnp.int32, z.shape, 0)
    zp1 = jnp.where(first_tile, 0.0, zp_ref[SUBLANES - 1:SUBLANES, :])
    zp2 = jnp.where(first_tile, 0.0, zp_ref[SUBLANES - 2:SUBLANES - 1, :])
    z1 = jnp.where(row == 0, zp1, pltpu.roll(z, 1, 0))
    z2 = jnp.where(row == 0, zp2, jnp.where(row == 1, zp1, pltpu.roll(z, 2, 0)))
    conv = cw_ref[0:1, :] * z2 + cw_ref[1:2, :] * z1 + cw_ref[2:3, :] * z
    y = gb_ref[...] * conv

    na = _rms_norm(attn_ref[...], ga_ref[...]).astype(jnp.bfloat16)
    nc = _rms_norm(y, gc_ref[...]).astype(jnp.bfloat16)
    o_ref[...] = h_ref[...] + _dot(na, wo_ref[0:attn_w, :]) + _dot(nc, wo_ref[attn_w:, :])


def _mix(sinks, h, q, k, v, gb, z, conv_w, g_attn, g_conv, w_out, *, seq):
    n, d = h.shape
    attn_w = q.shape[1]
    kv_w = k.shape[1]
    cw = z.shape[1]
    tile = TOKEN_TILE
    blocks_per_tile = tile // BLOCK
    rows8_per_tile = tile // SUBLANES
    streamed = tile * (2 * d * 4 + 2 * attn_w + 4 * kv_w + 2 * cw * 4) + 2 * BLOCK * kv_w * 2 + SUBLANES * cw * 4
    temps = tile * (attn_w * 4 * 3 + cw * 4 * 5 + d * 4) + 64 * BLOCK * 2 * BLOCK * 4
    return pl.pallas_call(
        functools.partial(_mix_kernel, tiles_per_seq=seq // tile, attn_w=attn_w),
        grid=(n // tile,),
        in_specs=[
            pl.BlockSpec(memory_space=pltpu.SMEM),
            _rows(tile, d),
            _rows(tile, attn_w),
            _rows(tile, kv_w),
            pl.BlockSpec((BLOCK, kv_w), lambda i: (jnp.maximum(i * blocks_per_tile - 1, 0), 0)),
            _rows(tile, kv_w),
            pl.BlockSpec((BLOCK, kv_w), lambda i: (jnp.maximum(i * blocks_per_tile - 1, 0), 0)),
            _rows(tile, cw),
            _rows(tile, cw),
            pl.BlockSpec((SUBLANES, cw), lambda i: (jnp.maximum(i * rows8_per_tile - 1, 0), 0)),
            _resident(conv_w.shape),
            _resident((1, attn_w)),
            _resident((1, cw)),
            _resident(w_out.shape),
        ],
        out_specs=_rows(tile, d),
        out_shape=jax.ShapeDtypeStruct((n, d), jnp.float32),
        scratch_shapes=[pltpu.VMEM((tile, attn_w), jnp.float32)],
        compiler_params=pltpu.CompilerParams(
            dimension_semantics=("arbitrary",),
            vmem_limit_bytes=_vmem_limit(2 * w_out.size, streamed, temps)),
        name="mix",
    )(sinks, h, q, k, k, v, v, gb, z, z, conv_w, g_attn, g_conv, w_out)


def _xattn_kernel(h_ref, g_ref, wq_ref, kt_ref, v_ref, wo_ref, o_ref, *, heads):
    x = h_ref[...]
    d = x.shape[1]
    hd = d // heads
    u = _rms_norm(x, g_ref[...]).astype(jnp.bfloat16)
    q = (_dot(u, wq_ref[...]) * (hd ** -0.5)).astype(jnp.bfloat16)
    outs = []
    for hh in range(heads):
        sc = _dot(q[:, hh * hd:(hh + 1) * hd], kt_ref[0, hh * hd:(hh + 1) * hd, :])
        m = jnp.max(sc, axis=-1, keepdims=True)
        p = jnp.exp(sc - m)
        inv = 1.0 / jnp.sum(p, axis=-1, keepdims=True)
        outs.append((_dot(p.astype(jnp.bfloat16), v_ref[0, :, hh * hd:(hh + 1) * hd]) * inv).astype(jnp.bfloat16))
    o = jnp.concatenate(outs, axis=1)
    o_ref[...] = x + _dot(o, wo_ref[...])


def _xattn(h, gain, wq, kt, v, wo, *, seq):
    n, d = h.shape
    m = v.shape[1]
    tile = TOKEN_TILE
    tiles_per_seq = seq // tile
    resident = 2 * (wq.size + wo.size)
    streamed = 2 * tile * d * 4 + 2 * 2 * m * d
    temps = tile * (d * 4 * 3 + d * 2 * 3 + m * 4 * 3)
    return pl.pallas_call(
        functools.partial(_xattn_kernel, heads=X_HEADS),
        grid=(n // tile,),
        in_specs=[
            _rows(tile, d),
            _resident((1, d)),
            _resident(wq.shape),
            pl.BlockSpec((1, d, m), lambda i: (i // tiles_per_seq, 0, 0)),
            pl.BlockSpec((1, m, d), lambda i: (i // tiles_per_seq, 0, 0)),
            _resident(wo.shape),
        ],
        out_specs=_rows(tile, d),
        out_shape=jax.ShapeDtypeStruct((n, d), jnp.float32),
        compiler_params=pltpu.CompilerParams(
            dimension_semantics=("arbitrary",),
            vmem_limit_bytes=_vmem_limit(resident, streamed, temps)),
        name="xattn",
    )(h, gain, wq, kt, v, wo)


def _slab_order():
    cols = []
    for s in range(Q_PER_KV):
        for head in (s, Q_PER_KV + s):
            cols.extend(range(head * HEAD_DIM, (head + 1) * HEAD_DIM))
    return jnp.asarray(cols, jnp.int32)


def kernel(x, mem, positions, g_ffn1, w_ffn1_in, w_ffn1_out, g_mix, w_mix_in, sinks, conv_w, g_attn_out, g_conv_out, w_mix_out, g_mem, g_xattn, w_xq, w_xkv, w_xo, g_ffn2, w_ffn2_in, w_ffn2_out, g_final):
    b, s, d = x.shape
    n = b * s
    depth = g_ffn1.shape[0]
    attn_w = N_Q_HEADS * HEAD_DIM
    kv_w = N_KV_HEADS * HEAD_DIM
    cw = conv_w.shape[2]
    d_ff = w_ffn1_out.shape[1]
    assert depth >= 1 and s % TOKEN_TILE == 0 and TOKEN_TILE % BLOCK == 0 and WINDOW <= BLOCK
    assert attn_w % LANES == 0 and kv_w == LANES and w_mix_in.shape[2] == attn_w + 2 * kv_w + 3 * cw

    bf = jnp.bfloat16
    row = lambda g: g.reshape(1, -1).astype(jnp.float32)
    order = _slab_order()
    half = HEAD_DIM // 2
    inv_freq = ROPE_THETA ** (-jnp.arange(half, dtype=jnp.float32) / half)
    freq = jnp.tile(inv_freq, LANES // half).reshape(1, LANES)
    pos = positions.reshape(n, 1).astype(jnp.int32)
    ones = jnp.ones((1, d), jnp.float32)

    h = x.reshape(n, d)
    for l in range(depth):
        last = l == depth - 1
        h = _ffn(h, row(g_ffn1[l]), w_ffn1_in[l][:, :d_ff].astype(bf), w_ffn1_in[l][:, d_ff:].astype(bf),
                 w_ffn1_out[l].astype(bf), ones, final_norm=False)

        w_in = w_mix_in[l]
        w_in = jnp.concatenate([w_in[:, :attn_w][:, order], w_in[:, attn_w:]], axis=1).astype(bf)
        q, k, v, gb, z = _mix_in(h, row(g_mix[l]), w_in, pos, freq, attn_w=attn_w, kv_w=kv_w, conv_w=cw)

        w_out = w_mix_out[l]
        w_out = jnp.concatenate([w_out[:attn_w][order], w_out[attn_w:]], axis=0).astype(bf)
        h = _mix(sinks[l].astype(jnp.float32), h, q, k, v, gb, z, conv_w[l].astype(jnp.float32),
                 row(g_attn_out[l][order]), row(g_conv_out[l]), w_out, seq=s)

        kt, vm = _memkv(mem, row(g_mem[l]), w_xkv[l].astype(bf))
        h = _xattn(h, row(g_xattn[l]), w_xq[l].astype(bf), kt, vm, w_xo[l].astype(bf), seq=s)

        h = _ffn(h, row(g_ffn2[l]), w_ffn2_in[l][:, :d_ff].astype(bf), w_ffn2_in[l][:, d_ff:].astype(bf),
                 w_ffn2_out[l].astype(bf), row(g_final) if last else ones, final_norm=last)
    return h.reshape(b, s, d)
```
